```python
import jax, jax.numpy as jnp
from jax import lax
import numpy as np

D_MODEL = 1024
BATCH = 8
SEQ = 2048
DEPTH = 1

HEAD_DIM = 64
N_Q_HEADS = 8
N_KV_HEADS = 2
Q_PER_KV = N_Q_HEADS // N_KV_HEADS
ATTN_WIDTH = N_Q_HEADS * HEAD_DIM
KV_WIDTH = N_KV_HEADS * HEAD_DIM
WINDOW = 128
BLOCK = 128
ROPE_THETA = 500000.0
ROT_DIM = HEAD_DIM // 4
POOL_WINDOWS = (2, 4, 8, 16)
N_POOL_GROUPS = len(POOL_WINDOWS)
POOL_WIDTH = D_MODEL - ATTN_WIDTH
POOL_GROUP_WIDTH = POOL_WIDTH // N_POOL_GROUPS
MIX_WIDTH = ATTN_WIDTH + POOL_WIDTH
IN_PROJ_WIDTH = ATTN_WIDTH + 2 * KV_WIDTH + POOL_WIDTH
D_FF = 2816
FFN_RESIDUAL_WEIGHT = 0.5
LN_EPS = 1e-5
DEEPNORM_ALPHA = (2.0 * DEPTH) ** 0.25
DEEPNORM_BETA = (8.0 * DEPTH) ** -0.25
N_SUBLAYERS = 3
ADA_WIDTH = 3 * N_SUBLAYERS * D_MODEL
NEG_INF = -1e30

kernel_name = "hymba_swa_sink_multiscale_pool_macaron_deepnorm_adaln"


def layer_norm(x, g, b):
    xf = x.astype(jnp.float32)
    mu = jnp.mean(xf, axis=-1, keepdims=True)
    var = jnp.mean(jnp.square(xf - mu), axis=-1, keepdims=True)
    y = (xf - mu) * lax.rsqrt(var + LN_EPS)
    return (y * g.astype(jnp.float32) + b.astype(jnp.float32)).astype(x.dtype)


def modulate(h, shift, scale):
    return h * (1.0 + scale[:, None, :]) + shift[:, None, :]


def swiglu(u, w1, w3, w2):
    return (jax.nn.silu(u @ w1) * (u @ w3)) @ w2


def partial_rope(t):
    s = t.shape[1]
    inv_freq = ROPE_THETA ** (-jnp.arange(0, ROT_DIM, 2, dtype=jnp.float32) / ROT_DIM)
    ang = jnp.arange(s, dtype=jnp.float32)[:, None] * inv_freq[None, :]
    cos = jnp.cos(ang)[None, :, None, :]
    sin = jnp.sin(ang)[None, :, None, :]
    tr = t[..., :ROT_DIM].astype(jnp.float32)
    t1, t2 = tr[..., :ROT_DIM // 2], tr[..., ROT_DIM // 2:]
    rot = jnp.concatenate([t1 * cos - t2 * sin, t2 * cos + t1 * sin], axis=-1).astype(t.dtype)
    return jnp.concatenate([rot, t[..., ROT_DIM:]], axis=-1)


def sliding_window_attention(q, k, v, sinks):
    b, s = q.shape[0], q.shape[1]
    nb = s // BLOCK
    qb = q.reshape(b, nb, BLOCK, N_KV_HEADS, Q_PER_KV, HEAD_DIM)

    def band(t):
        tb = t.reshape(b, nb, BLOCK, N_KV_HEADS, HEAD_DIM)
        prev = jnp.pad(tb, ((0, 0), (1, 0), (0, 0), (0, 0), (0, 0)))[:, :-1]
        return jnp.concatenate([prev, tb], axis=2)

    kb, vb = band(k), band(v)
    scores = jnp.einsum('bnqhgd,bnkhd->bnhgqk', qb, kb).astype(jnp.float32) * (HEAD_DIM ** -0.5)
    blk = jnp.arange(nb)[:, None, None]
    qpos = blk * BLOCK + jnp.arange(BLOCK)[None, :, None]
    kpos = (blk - 1) * BLOCK + jnp.arange(2 * BLOCK)[None, None, :]
    rel = qpos - kpos
    valid = (rel >= 0) & (rel < WINDOW) & (kpos >= 0)
    scores = jnp.where(valid[None, :, None, None], scores, NEG_INF)
    sink = sinks.astype(jnp.float32).reshape(N_KV_HEADS, Q_PER_KV)[None, None, :, :, None, None]
    sink = jnp.broadcast_to(sink, scores.shape[:-1] + (1,))
    probs = jax.nn.softmax(jnp.concatenate([scores, sink], axis=-1), axis=-1)[..., :-1]
    out = jnp.einsum('bnhgqk,bnkhd->bnqhgd', probs.astype(v.dtype), vb)
    return out.reshape(b, s, N_Q_HEADS * HEAD_DIM)


def multiscale_pool(p, w_pool, pool_scale):
    b, s, _ = p.shape
    pf = p.astype(jnp.float32)
    cs = jnp.pad(jnp.cumsum(pf, axis=1), ((0, 0), (1, 0), (0, 0)))
    t = jnp.arange(s)
    means = []
    for g, w in enumerate(POOL_WINDOWS):
        sl = slice(g * POOL_GROUP_WIDTH, (g + 1) * POOL_GROUP_WIDTH)
        start = jnp.maximum(t + 1 - w, 0)
        window_sum = cs[:, t + 1, sl] - cs[:, start, sl]
        count = (t + 1 - start).astype(jnp.float32)
        means.append(window_sum / count[None, :, None])
    pooled = jnp.stack(means, axis=2)
    diff = (pooled - pf.reshape(b, s, N_POOL_GROUPS, POOL_GROUP_WIDTH)).astype(p.dtype)
    mixed = jnp.einsum('bsgc,gcd->bsgd', diff, w_pool).reshape(b, s, POOL_WIDTH)
    return mixed * pool_scale


def setup_inputs(seed: int = 0) -> dict:
    key = jax.random.key(seed)
    ks = jax.random.split(key, 24)
    L, D = DEPTH, D_MODEL

    def nrm(k, shape, scale):
        return jax.random.normal(k, shape, jnp.float32) * scale

    v_cols = jnp.ones((IN_PROJ_WIDTH,), jnp.float32).at[
        ATTN_WIDTH + KV_WIDTH:ATTN_WIDTH + 2 * KV_WIDTH].set(DEEPNORM_BETA)
    return {
        "x": nrm(ks[0], (BATCH, SEQ, D), 1.0),
        "c": nrm(ks[1], (BATCH, D), 1.0),
        "w_ada": nrm(ks[2], (L, D, ADA_WIDTH), 0.1 * D ** -0.5),
        "b_ada": nrm(ks[3], (L, ADA_WIDTH), 0.01),
        "ffn1_w1": nrm(ks[4], (L, D, D_FF), D ** -0.5),
        "ffn1_w3": nrm(ks[5], (L, D, D_FF), D ** -0.5),
        "ffn1_w2": nrm(ks[6], (L, D_FF, D), DEEPNORM_BETA * D_FF ** -0.5),
        "ln1_g": 1.0 + nrm(ks[7], (L, D), 0.05),
        "ln1_b": nrm(ks[8], (L, D), 0.05),
        "w_in": nrm(ks[9], (L, D, IN_PROJ_WIDTH), D ** -0.5) * v_cols,
        "b_in": nrm(ks[10], (L, IN_PROJ_WIDTH), 0.01),
        "attn_sinks": nrm(ks[11], (L, N_Q_HEADS), 0.5),
        "w_pool": nrm(ks[12], (L, N_POOL_GROUPS, POOL_GROUP_WIDTH, POOL_GROUP_WIDTH), POOL_GROUP_WIDTH ** -0.5),
        "pool_scale": 1.0 + nrm(ks[13], (L, POOL_WIDTH), 0.1),
        "w_out": nrm(ks[14], (L, MIX_WIDTH, D), DEEPNORM_BETA * MIX_WIDTH ** -0.5),
        "b_out": nrm(ks[15], (L, D), 0.01),
        "ln2_g": 1.0 + nrm(ks[16], (L, D), 0.05),
        "ln2_b": nrm(ks[17], (L, D), 0.05),
        "ffn2_w1": nrm(ks[18], (L, D, D_FF), D ** -0.5),
        "ffn2_w3": nrm(ks[19], (L, D, D_FF), D ** -0.5),
        "ffn2_w2": nrm(ks[20], (L, D_FF, D), DEEPNORM_BETA * D_FF ** -0.5),
        "ln3_g": 1.0 + nrm(ks[21], (L, D), 0.05),
        "ln3_b": nrm(ks[22], (L, D), 0.05),
    }


def reference(x, c, w_ada, b_ada, ffn1_w1, ffn1_w3, ffn1_w2, ln1_g, ln1_b,
              w_in, b_in, attn_sinks, w_pool, pool_scale, w_out, b_out, ln2_g, ln2_b,
              ffn2_w1, ffn2_w3, ffn2_w2, ln3_g, ln3_b):
    b, s, _ = x.shape
    h = x
    c_act = jax.nn.silu(c)
    for l in range(DEPTH):
        mods = (c_act @ w_ada[l] + b_ada[l]).astype(h.dtype)
        (sh1, sc1, gt1, sh2, sc2, gt2, sh3, sc3, gt3) = jnp.split(mods, 3 * N_SUBLAYERS, axis=-1)

        u = modulate(h, sh1, sc1)
        f = swiglu(u, ffn1_w1[l], ffn1_w3[l], ffn1_w2[l])
        h = layer_norm(DEEPNORM_ALPHA * h + FFN_RESIDUAL_WEIGHT * (1.0 + gt1[:, None, :]) * f,
                       ln1_g[l], ln1_b[l])

        u = modulate(h, sh2, sc2)
        z = u @ w_in[l] + b_in[l]
        q = z[..., :ATTN_WIDTH].reshape(b, s, N_Q_HEADS, HEAD_DIM)
        k = z[..., ATTN_WIDTH:ATTN_WIDTH + KV_WIDTH].reshape(b, s, N_KV_HEADS, HEAD_DIM)
        v = z[..., ATTN_WIDTH + KV_WIDTH:ATTN_WIDTH + 2 * KV_WIDTH].reshape(b, s, N_KV_HEADS, HEAD_DIM)
        p = z[..., ATTN_WIDTH + 2 * KV_WIDTH:]
        a = sliding_window_attention(partial_rope(q), partial_rope(k), v, attn_sinks[l])
        m = multiscale_pool(p, w_pool[l], pool_scale[l])
        y = jnp.concatenate([a, m], axis=-1) @ w_out[l] + b_out[l]
        h = layer_norm(DEEPNORM_ALPHA * h + (1.0 + gt2[:, None, :]) * y, ln2_g[l], ln2_b[l])

        u = modulate(h, sh3, sc3)
        f = swiglu(u, ffn2_w1[l], ffn2_w3[l], ffn2_w2[l])
        h = layer_norm(DEEPNORM_ALPHA * h + FFN_RESIDUAL_WEIGHT * (1.0 + gt3[:, None, :]) * f,
                       ln3_g[l], ln3_b[l])
    return h
```

```python
import functools

import jax
import jax.numpy as jnp
from jax import lax
from jax.experimental import pallas as pl
from jax.experimental.pallas import tpu as pltpu

D_MODEL = 1024
HEAD_DIM = 64
N_Q_HEADS = 8
N_KV_HEADS = 2
Q_PER_KV = N_Q_HEADS // N_KV_HEADS
ATTN_WIDTH = N_Q_HEADS * HEAD_DIM
KV_WIDTH = N_KV_HEADS * HEAD_DIM
WINDOW = 128
BLOCK = 128
ROPE_THETA = 500000.0
ROT_DIM = HEAD_DIM // 4
POOL_WINDOWS = (2, 4, 8, 16)
POOL_HALO = 16
POOL_WIDTH = D_MODEL - ATTN_WIDTH
POOL_GROUP_WIDTH = POOL_WIDTH // len(POOL_WINDOWS)
IN_PROJ_WIDTH = ATTN_WIDTH + 2 * KV_WIDTH + POOL_WIDTH
D_FF = 2816
FFN_RESIDUAL_WEIGHT = 0.5
LN_EPS = 1e-5
DEPTH = 1
DEEPNORM_ALPHA = (2.0 * DEPTH) ** 0.25
N_SUBLAYERS = 3
NEG_INF = -1e30

LANES = 128
VMEM_LIMIT_BYTES = 56 * 1024 * 1024

FFN_ROWS = 512
MIX_ROWS = 1024

_BF16 = jnp.bfloat16
_F32 = jnp.float32


def _dot(a, b):
    return jnp.dot(a, b, preferred_element_type=_F32)


def _layer_norm(r, g, b):
    mu = jnp.mean(r, axis=-1, keepdims=True)
    d = r - mu
    var = jnp.mean(d * d, axis=-1, keepdims=True)
    return d * lax.rsqrt(var + LN_EPS) * g + b


def _ada_kernel(c_ref, w_ref, b_ref, o_ref):
    c = c_ref[...]
    act = (c * jax.nn.sigmoid(c)).astype(_BF16)
    o_ref[...] = _dot(act, w_ref[...].astype(_BF16)) + b_ref[...]


def _ada_call(c, w_ada, b_ada):
    batch, d = c.shape
    width = w_ada.shape[1]
    bn = D_MODEL
    return pl.pallas_call(
        _ada_kernel,
        grid=(width // bn,),
        in_specs=[
            pl.BlockSpec((batch, d), lambda j: (0, 0)),
            pl.BlockSpec((d, bn), lambda j: (0, j)),
            pl.BlockSpec((1, bn), lambda j: (0, j)),
        ],
        out_specs=pl.BlockSpec((batch, bn), lambda j: (0, j)),
        out_shape=jax.ShapeDtypeStruct((batch, width), _F32),
        compiler_params=pltpu.CompilerParams(
            dimension_semantics=("arbitrary",), vmem_limit_bytes=VMEM_LIMIT_BYTES),
        name="ada_mods",
    )(c, w_ada, b_ada.reshape(1, width))


def _ffn_kernel(h_ref, mod_ref, w1_ref, w3_ref, w2_ref, g_ref, b_ref, o_ref, *, mod_row):
    h = h_ref[...]
    shift = mod_ref[0, mod_row:mod_row + 1, :]
    scale = mod_ref[0, mod_row + 1:mod_row + 2, :]
    gate = mod_ref[0, mod_row + 2:mod_row + 3, :]
    u = (h * (1.0 + scale) + shift).astype(_BF16)
    a = _dot(u, w1_ref[...])
    v = _dot(u, w3_ref[...])
    act = (a * jax.nn.sigmoid(a) * v).astype(_BF16)
    f = _dot(act, w2_ref[...])
    r = DEEPNORM_ALPHA * h + (FFN_RESIDUAL_WEIGHT * (1.0 + gate)) * f
    o_ref[...] = _layer_norm(r, g_ref[...], b_ref[...])


def _ffn_call(h2d, mods, w1, w3, w2, ln_g, ln_b, *, mod_row, seq):
    n, d = h2d.shape
    tm = FFN_ROWS
    resident = functools.partial(pl.BlockSpec, pipeline_mode=pl.Buffered(1))
    return pl.pallas_call(
        functools.partial(_ffn_kernel, mod_row=mod_row),
        grid=(n // tm,),
        in_specs=[
            pl.BlockSpec((tm, d), lambda t: (t, 0)),
            pl.BlockSpec((1, 3 * N_SUBLAYERS, d), lambda t: (t * tm // seq, 0, 0)),
            resident((d, D_FF), lambda t: (0, 0)),
            resident((d, D_FF), lambda t: (0, 0)),
            resident((D_FF, d), lambda t: (0, 0)),
            pl.BlockSpec((1, d), lambda t: (0, 0)),
            pl.BlockSpec((1, d), lambda t: (0, 0)),
        ],
        out_specs=pl.BlockSpec((tm, d), lambda t: (t, 0)),
        out_shape=jax.ShapeDtypeStruct((n, d), _F32),
        compiler_params=pltpu.CompilerParams(
            dimension_semantics=("arbitrary",), vmem_limit_bytes=VMEM_LIMIT_BYTES),
        name=f"ffn_sublayer_{mod_row // 3}",
    )(h2d, mods, w1, w3, w2, ln_g.reshape(1, d), ln_b.reshape(1, d))


def _rope(t, cos, sin_hi, sin_lo):
    half = ROT_DIM // 2
    return t * cos + pltpu.roll(t, half, 1) * sin_hi + pltpu.roll(t, LANES - half, 1) * sin_lo


def _mixer_kernel(sink_ref, h_ref, mod_ref, rope_ref, w_in_ref, b_in_ref, w_pool_ref, pscale_ref,
                  w_out_ref, b_out_ref, g_ref, b_ref, o_ref,
                  kd_ref, vd_ref, pool_ref, ql_ref, qh_ref, am_ref, *, mod_row):
    ts = h_ref.shape[0]
    st = pl.program_id(1)

    @pl.when(st == 0)
    def _():
        kd_ref[:, 0:BLOCK, :] = jnp.zeros((N_KV_HEADS, BLOCK, LANES), _BF16)
        vd_ref[:, 0:BLOCK, :] = jnp.zeros((N_KV_HEADS, BLOCK, LANES), _BF16)
        pool_ref[0:POOL_HALO, :] = jnp.zeros((POOL_HALO, POOL_WIDTH), _F32)

    h = h_ref[...]
    shift = mod_ref[0, mod_row:mod_row + 1, :]
    scale = mod_ref[0, mod_row + 1:mod_row + 2, :]
    gate = mod_ref[0, mod_row + 2:mod_row + 3, :]
    u = (h * (1.0 + scale) + shift).astype(_BF16)

    lane = lax.broadcasted_iota(jnp.int32, (ts, LANES), 1)
    low_half = lane < HEAD_DIM

    zq = _dot(u, w_in_ref[:, 0:ATTN_WIDTH]) + b_in_ref[:, 0:ATTN_WIDTH]
    for j in range(ATTN_WIDTH // LANES):
        sl = slice(j * LANES, (j + 1) * LANES)
        tr = _rope(zq[:, sl], rope_ref[0], rope_ref[1], rope_ref[2])
        ql_ref[:, sl] = jnp.where(low_half, tr, 0.0).astype(_BF16)
        qh_ref[:, sl] = jnp.where(low_half, 0.0, tr).astype(_BF16)

    zkv = _dot(u, w_in_ref[:, ATTN_WIDTH:ATTN_WIDTH + 2 * KV_WIDTH]) \
        + b_in_ref[:, ATTN_WIDTH:ATTN_WIDTH + 2 * KV_WIDTH]
    kr = _rope(zkv[:, 0:KV_WIDTH], rope_ref[3], rope_ref[4], rope_ref[5])
    vv = zkv[:, KV_WIDTH:2 * KV_WIDTH]
    for src, dst in ((kr, kd_ref), (vv, vd_ref)):
        swapped = pltpu.roll(src, HEAD_DIM, 1)
        dst[0, BLOCK:BLOCK + ts, :] = jnp.where(low_half, src, swapped).astype(_BF16)
        dst[1, BLOCK:BLOCK + ts, :] = jnp.where(low_half, swapped, src).astype(_BF16)

    pool_ref[POOL_HALO:POOL_HALO + ts, :] = _dot(u, w_in_ref[:, ATTN_WIDTH + 2 * KV_WIDTH:]) \
        + b_in_ref[:, ATTN_WIDTH + 2 * KV_WIDTH:]
    pos = st * ts + lax.broadcasted_iota(jnp.int32, (ts, 1), 0)
    for g, w in enumerate(POOL_WINDOWS):
        sl = slice(g * POOL_GROUP_WIDTH, (g + 1) * POOL_GROUP_WIDTH)
        ext = pool_ref[:, sl]
        acc = ext
        k = 1
        while k < w:
            acc = acc + pltpu.roll(acc, k, 0)
            k *= 2
        count = jnp.minimum(pos + 1, w).astype(_F32)
        diff = (acc[POOL_HALO:, :] / count - ext[POOL_HALO:, :]).astype(_BF16)
        mixed = _dot(diff, w_pool_ref[g]) * pscale_ref[:, sl]
        am_ref[:, ATTN_WIDTH + g * POOL_GROUP_WIDTH:ATTN_WIDTH + (g + 1) * POOL_GROUP_WIDTH] = \
            mixed.astype(_BF16)

    rows = Q_PER_KV * BLOCK
    ri = lax.broadcasted_iota(jnp.int32, (rows, 2 * BLOCK), 0)
    qi = ri & (BLOCK - 1)
    kj = lax.broadcasted_iota(jnp.int32, (rows, 2 * BLOCK), 1)
    band = (kj > qi) & (kj <= qi + WINDOW)
    band_first = band & ((kj >= BLOCK) | (st > 0))
    head_of_row = lax.broadcasted_iota(jnp.int32, (rows, 1), 0) // BLOCK
    o_low_half = lax.broadcasted_iota(jnp.int32, (BLOCK, LANES), 1) < HEAD_DIM
    for hk in range(N_KV_HEADS):
        sink = jnp.zeros((rows, 1), _F32)
        for gq in range(Q_PER_KV):
            sink = jnp.where(head_of_row == gq, sink_ref[hk * Q_PER_KV + gq], sink)
        for n in range(ts // BLOCK):
            r0 = n * BLOCK
            q_rows = slice(r0, r0 + BLOCK)
            parts = []
            for jj in range(Q_PER_KV // 2):
                sl = slice((hk * (Q_PER_KV // 2) + jj) * LANES, (hk * (Q_PER_KV // 2) + jj + 1) * LANES)
                parts += [ql_ref[q_rows, sl], qh_ref[q_rows, sl]]
            qs = jnp.concatenate(parts, axis=0)
            kb = kd_ref[hk, r0:r0 + 2 * BLOCK, :]
            s = lax.dot_general(qs, kb, (((1,), (1,)), ((), ())), preferred_element_type=_F32)
            s = jnp.where(band_first if n == 0 else band, s, NEG_INF)
            m = jnp.maximum(jnp.max(s, axis=-1, keepdims=True), sink)
            e = jnp.exp(s - m)
            den = jnp.sum(e, axis=-1, keepdims=True) + jnp.exp(sink - m)
            o = _dot(e.astype(_BF16), vd_ref[hk, r0:r0 + 2 * BLOCK, :]) / den
            for jj in range(Q_PER_KV // 2):
                lo = o[(2 * jj) * BLOCK:(2 * jj + 1) * BLOCK, :]
                hi = o[(2 * jj + 1) * BLOCK:(2 * jj + 2) * BLOCK, :]
                col = (hk * (Q_PER_KV // 2) + jj) * LANES
                am_ref[q_rows, col:col + LANES] = jnp.where(o_low_half, lo, hi).astype(_BF16)

    y = _dot(am_ref[...], w_out_ref[...]) + b_out_ref[...]
    r = DEEPNORM_ALPHA * h + (1.0 + gate) * y
    o_ref[...] = _layer_norm(r, g_ref[...], b_ref[...])

    kd_ref[:, 0:BLOCK, :] = kd_ref[:, ts:ts + BLOCK, :]
    vd_ref[:, 0:BLOCK, :] = vd_ref[:, ts:ts + BLOCK, :]
    pool_ref[0:POOL_HALO, :] = pool_ref[ts:ts + POOL_HALO, :]


def _rope_tables(seq):
    half = ROT_DIM // 2
    inv_freq = ROPE_THETA ** (-jnp.arange(0, ROT_DIM, 2, dtype=_F32) / ROT_DIM)
    ang = jnp.arange(seq, dtype=_F32)[:, None] * inv_freq[None, :]
    cos, sin = jnp.cos(ang), jnp.sin(ang)
    d = jnp.arange(LANES) % HEAD_DIM
    f = d % half
    c_tab = jnp.where(d[None, :] < ROT_DIM, cos[:, f], 1.0)
    s_hi = jnp.where((d[None, :] >= half) & (d[None, :] < ROT_DIM), sin[:, f], 0.0)
    s_lo = jnp.where(d[None, :] < half, -sin[:, f], 0.0)
    k_tabs = jnp.stack([c_tab, s_hi, s_lo])
    return jnp.concatenate([k_tabs * (HEAD_DIM ** -0.5), k_tabs], axis=0).astype(_F32)


def _mixer_call(h2d, mods, rope, sinks, w_in, b_in, w_pool, pool_scale, w_out, b_out, ln_g, ln_b,
                *, mod_row, batch, seq):
    n, d = h2d.shape
    ts = MIX_ROWS
    tiles = seq // ts
    resident = functools.partial(pl.BlockSpec, pipeline_mode=pl.Buffered(1))
    const2 = lambda b, s: (0, 0)
    return pl.pallas_call(
        functools.partial(_mixer_kernel, mod_row=mod_row),
        grid=(batch, tiles),
        in_specs=[
            pl.BlockSpec(memory_space=pltpu.SMEM),
            pl.BlockSpec((ts, d), lambda b, s: (b * tiles + s, 0)),
            pl.BlockSpec((1, 3 * N_SUBLAYERS, d), lambda b, s: (b, 0, 0)),
            pl.BlockSpec((6, ts, LANES), lambda b, s: (0, s, 0)),
            resident((d, IN_PROJ_WIDTH), const2),
            pl.BlockSpec((1, IN_PROJ_WIDTH), const2),
            resident((len(POOL_WINDOWS), POOL_GROUP_WIDTH, POOL_GROUP_WIDTH), lambda b, s: (0, 0, 0)),
            pl.BlockSpec((1, POOL_WIDTH), const2),
            resident((d, d), const2),
            pl.BlockSpec((1, d), const2),
            pl.BlockSpec((1, d), const2),
            pl.BlockSpec((1, d), const2),
        ],
        out_specs=pl.BlockSpec((ts, d), lambda b, s: (b * tiles + s, 0)),
        out_shape=jax.ShapeDtypeStruct((n, d), _F32),
        scratch_shapes=[
            pltpu.VMEM((N_KV_HEADS, BLOCK + ts, LANES), _BF16),
            pltpu.VMEM((N_KV_HEADS, BLOCK + ts, LANES), _BF16),
            pltpu.VMEM((POOL_HALO + ts, POOL_WIDTH), _F32),
            pltpu.VMEM((ts, ATTN_WIDTH), _BF16),
            pltpu.VMEM((ts, ATTN_WIDTH), _BF16),
            pltpu.VMEM((ts, d), _BF16),
        ],
        compiler_params=pltpu.CompilerParams(
            dimension_semantics=("arbitrary", "arbitrary"), vmem_limit_bytes=VMEM_LIMIT_BYTES),
        name="mixer_sublayer",
    )(sinks, h2d, mods, rope, w_in, b_in.reshape(1, -1), w_pool, pool_scale.reshape(1, -1),
      w_out, b_out.reshape(1, -1), ln_g.reshape(1, d), ln_b.reshape(1, d))


def kernel(x, c, w_ada, b_ada, ffn1_w1, ffn1_w3, ffn1_w2, ln1_g, ln1_b, w_in, b_in, attn_sinks, w_pool, pool_scale, w_out, b_out, ln2_g, ln2_b, ffn2_w1, ffn2_w3, ffn2_w2, ln3_g, ln3_b):
    batch, seq, d = x.shape
    assert d == D_MODEL and seq % MIX_ROWS == 0 and (batch * seq) % FFN_ROWS == 0 and seq % FFN_ROWS == 0
    assert w_ada.shape[0] == DEPTH
    h = x.reshape(batch * seq, d)
    rope = _rope_tables(seq)
    for l in range(DEPTH):
        mods = _ada_call(c, w_ada[l], b_ada[l]).reshape(batch, 3 * N_SUBLAYERS, d)
        h = _ffn_call(h, mods, ffn1_w1[l].astype(_BF16), ffn1_w3[l].astype(_BF16),
                      ffn1_w2[l].astype(_BF16), ln1_g[l], ln1_b[l], mod_row=0, seq=seq)
        h = _mixer_call(h, mods, rope, attn_sinks[l], w_in[l].astype(_BF16), b_in[l],
                        w_pool[l].astype(_BF16), pool_scale[l], w_out[l].astype(_BF16), b_out[l],
                        ln2_g[l], ln2_b[l], mod_row=3, batch=batch, seq=seq)
        h = _ffn_call(h, mods, ffn2_w1[l].astype(_BF16), ffn2_w3[l].astype(_BF16),
                      ffn2_w2[l].astype(_BF16), ln3_g[l], ln3_b[l], mod_row=6, seq=seq)
    return h.reshape(batch, seq, d)
```

```python
import functools

import jax
import jax.numpy as jnp
from jax import lax
from jax.experimental import pallas as pl
from jax.experimental.pallas import tpu as pltpu

D_MODEL = 1024
HEAD_DIM = 64
N_Q_HEADS = 8
N_KV_HEADS = 2
Q_PER_KV = N_Q_HEADS // N_KV_HEADS
ATTN_WIDTH = N_Q_HEADS * HEAD_DIM
KV_WIDTH = N_KV_HEADS * HEAD_DIM
WINDOW = 128
BLOCK = 128
ROPE_THETA = 500000.0
ROT_DIM = HEAD_DIM // 4
POOL_WINDOWS = (2, 4, 8, 16)
POOL_HALO = 16
POOL_WIDTH = D_MODEL - ATTN_WIDTH
POOL_GROUP_WIDTH = POOL_WIDTH // len(POOL_WINDOWS)
IN_PROJ_WIDTH = ATTN_WIDTH + 2 * KV_WIDTH + POOL_WIDTH
D_FF = 2816
FFN_RESIDUAL_WEIGHT = 0.5
LN_EPS = 1e-5
DEPTH = 1
DEEPNORM_ALPHA = (2.0 * DEPTH) ** 0.25
N_SUBLAYERS = 3
NEG_INF = -1e30
LOG2_E = 1.4426950408889634

LANES = 128
VMEM_LIMIT_BYTES = 56 * 1024 * 1024

FFN_ROWS = 512
MIX_ROWS = 1024

_BF16 = jnp.bfloat16
_F32 = jnp.float32


def _dot(a, b):
    return jnp.dot(a, b, preferred_element_type=_F32)


def _layer_norm(r, g, b):
    mu = jnp.mean(r, axis=-1, keepdims=True)
    d = r - mu
    var = jnp.mean(d * d, axis=-1, keepdims=True)
    return d * lax.rsqrt(var + LN_EPS) * g + b


def _ada_kernel(c_ref, w_ref, b_ref, o_ref):
    c = c_ref[...]
    act = (c * jax.nn.sigmoid(c)).astype(_BF16)
    o_ref[...] = _dot(act, w_ref[...].astype(_BF16)) + b_ref[...]


def _ada_call(c, w_ada, b_ada):
    batch, d = c.shape
    width = w_ada.shape[1]
    bn = D_MODEL
    return pl.pallas_call(
        _ada_kernel,
        grid=(width // bn,),
        in_specs=[
            pl.BlockSpec((batch, d), lambda j: (0, 0)),
            pl.BlockSpec((d, bn), lambda j: (0, j)),
            pl.BlockSpec((1, bn), lambda j: (0, j)),
        ],
        out_specs=pl.BlockSpec((batch, bn), lambda j: (0, j)),
        out_shape=jax.ShapeDtypeStruct((batch, width), _F32),
        compiler_params=pltpu.CompilerParams(
            dimension_semantics=("arbitrary",), vmem_limit_bytes=VMEM_LIMIT_BYTES),
        name="ada_mods",
    )(c, w_ada, b_ada.reshape(1, width))


def _ffn_kernel(h_ref, mod_ref, w1_ref, w3_ref, w2_ref, g_ref, b_ref, o_ref, r_ref, *, mod_row, n_tiles):
    t = pl.program_id(0)

    @pl.when(t == 0)
    def _():
        r_ref[...] = jnp.zeros(r_ref.shape, _F32)

    def finish_previous_tile():
        o_ref[...] = _layer_norm(r_ref[...], g_ref[...], b_ref[...])

    @pl.when(t < n_tiles)
    def _():
        finish_previous_tile()
        h = h_ref[...]
        shift = mod_ref[0, mod_row:mod_row + 1, :]
        scale = mod_ref[0, mod_row + 1:mod_row + 2, :]
        gate = mod_ref[0, mod_row + 2:mod_row + 3, :]
        u = (h * (1.0 + scale) + shift).astype(_BF16)
        a = _dot(u, w1_ref[...])
        v = _dot(u, w3_ref[...])
        act = (a * jax.nn.sigmoid(a) * v).astype(_BF16)
        f = _dot(act, w2_ref[...])
        r_ref[...] = DEEPNORM_ALPHA * h + (FFN_RESIDUAL_WEIGHT * (1.0 + gate)) * f

    @pl.when(t == n_tiles)
    def _():
        finish_previous_tile()


def _ffn_call(h2d, mods, w1, w3, w2, ln_g, ln_b, *, mod_row, seq):
    n, d = h2d.shape
    tm = FFN_ROWS
    n_tiles = n // tm
    resident = functools.partial(pl.BlockSpec, pipeline_mode=pl.Buffered(1))
    cur = lambda t: jnp.minimum(t, n_tiles - 1)
    return pl.pallas_call(
        functools.partial(_ffn_kernel, mod_row=mod_row, n_tiles=n_tiles),
        grid=(n_tiles + 1,),
        in_specs=[
            pl.BlockSpec((tm, d), lambda t: (cur(t), 0)),
            pl.BlockSpec((1, 3 * N_SUBLAYERS, d), lambda t: (cur(t) * tm // seq, 0, 0)),
            resident((d, D_FF), lambda t: (0, 0)),
            resident((d, D_FF), lambda t: (0, 0)),
            resident((D_FF, d), lambda t: (0, 0)),
            pl.BlockSpec((1, d), lambda t: (0, 0)),
            pl.BlockSpec((1, d), lambda t: (0, 0)),
        ],
        out_specs=pl.BlockSpec((tm, d), lambda t: (jnp.maximum(t - 1, 0), 0)),
        out_shape=jax.ShapeDtypeStruct((n, d), _F32),
        scratch_shapes=[pltpu.VMEM((tm, d), _F32)],
        compiler_params=pltpu.CompilerParams(
            dimension_semantics=("arbitrary",), vmem_limit_bytes=VMEM_LIMIT_BYTES),
        name=f"ffn_sublayer_{mod_row // 3}",
    )(h2d, mods, w1, w3, w2, ln_g.reshape(1, d), ln_b.reshape(1, d))


def _rope(t, cos, sin_hi, sin_lo):
    half = ROT_DIM // 2
    return t * cos + pltpu.roll(t, half, 1) * sin_hi + pltpu.roll(t, LANES - half, 1) * sin_lo


def _mixer_kernel(sink_ref, h_ref, mod_ref, rope_ref, w_in_ref, b_in_ref, w_pool_ref, pscale_ref,
                  w_out_ref, b_out_ref, g_ref, b_ref, o_ref,
                  kd_ref, vd_ref, pool_ref, ql_ref, qh_ref, ao_ref, am_ref, cap_ref, r_ref,
                  *, mod_row, n_tiles, seq_tiles):
    ts = h_ref.shape[0]
    t = pl.program_id(0)
    st = t % seq_tiles

    @pl.when(t == 0)
    def _():
        r_ref[...] = jnp.zeros(r_ref.shape, _F32)

    @pl.when(st == 0)
    def _():
        kd_ref[:, 0:BLOCK, :] = jnp.zeros((N_KV_HEADS, BLOCK, LANES), _BF16)
        vd_ref[:, 0:BLOCK, :] = jnp.zeros((N_KV_HEADS, BLOCK, 2 * LANES), _BF16)
        pool_ref[0:POOL_HALO, :] = jnp.zeros((POOL_HALO, POOL_WIDTH), _F32)

    pairs = Q_PER_KV // 2

    def finish_previous_tile():
        o_ref[...] = _layer_norm(r_ref[...], g_ref[...], b_ref[...])

    def project_and_pool():
        h = h_ref[...]
        shift = mod_ref[0, mod_row:mod_row + 1, :]
        scale = mod_ref[0, mod_row + 1:mod_row + 2, :]
        u = (h * (1.0 + scale) + shift).astype(_BF16)

        lane = lax.broadcasted_iota(jnp.int32, (ts, LANES), 1)
        low_half = lane < HEAD_DIM

        zq = _dot(u, w_in_ref[:, 0:ATTN_WIDTH]) + b_in_ref[:, 0:ATTN_WIDTH]
        finish_previous_tile()
        for j in range(ATTN_WIDTH // LANES):
            tr = _rope(zq[:, j * LANES:(j + 1) * LANES], rope_ref[0], rope_ref[1], rope_ref[2])
            hk, sl = j // pairs, slice((j % pairs) * LANES, (j % pairs + 1) * LANES)
            ql_ref[hk, :, sl] = jnp.where(low_half, tr, 0.0).astype(_BF16)
            qh_ref[hk, :, sl] = jnp.where(low_half, 0.0, tr).astype(_BF16)

        zkv = _dot(u, w_in_ref[:, ATTN_WIDTH:ATTN_WIDTH + 2 * KV_WIDTH]) \
            + b_in_ref[:, ATTN_WIDTH:ATTN_WIDTH + 2 * KV_WIDTH]
        kr = _rope(zkv[:, 0:KV_WIDTH], rope_ref[3], rope_ref[4], rope_ref[5])
        vv = zkv[:, KV_WIDTH:2 * KV_WIDTH]
        for src, dst in ((kr, kd_ref), (vv, vd_ref)):
            swapped = pltpu.roll(src, HEAD_DIM, 1)
            dst[0, BLOCK:BLOCK + ts, 0:LANES] = jnp.where(low_half, src, swapped).astype(_BF16)
            dst[1, BLOCK:BLOCK + ts, 0:LANES] = jnp.where(low_half, swapped, src).astype(_BF16)
        vd_ref[:, BLOCK:BLOCK + ts, LANES:2 * LANES] = jnp.ones((N_KV_HEADS, ts, LANES), _BF16)

        pool_ref[POOL_HALO:POOL_HALO + ts, :] = _dot(u, w_in_ref[:, ATTN_WIDTH + 2 * KV_WIDTH:]) \
            + b_in_ref[:, ATTN_WIDTH + 2 * KV_WIDTH:]
        head_pos = st * ts + lax.broadcasted_iota(jnp.int32, (POOL_HALO, 1), 0)
        for g, w in enumerate(POOL_WINDOWS):
            sl = slice(g * POOL_GROUP_WIDTH, (g + 1) * POOL_GROUP_WIDTH)
            ext = pool_ref[:, sl]
            acc = ext
            k = 1
            while k < w:
                acc = acc + pltpu.roll(acc, k, 0)
                k *= 2
            head_count = jnp.minimum(head_pos + 1, w).astype(_F32)
            mean = jnp.concatenate([acc[POOL_HALO:2 * POOL_HALO, :] / head_count,
                                    acc[2 * POOL_HALO:, :] * (1.0 / w)], axis=0)
            diff = (mean - ext[POOL_HALO:, :]).astype(_BF16)
            mixed = _dot(diff, w_pool_ref[g]) * pscale_ref[:, sl]
            am_ref[:, ATTN_WIDTH + g * POOL_GROUP_WIDTH:ATTN_WIDTH + (g + 1) * POOL_GROUP_WIDTH] = \
                mixed.astype(_BF16)

        qi = lax.broadcasted_iota(jnp.int32, (BLOCK, 2 * BLOCK), 0)
        kj = lax.broadcasted_iota(jnp.int32, (BLOCK, 2 * BLOCK), 1)
        band = (kj > qi) & (kj <= qi + WINDOW)
        keep = float(jnp.finfo(_F32).max)
        cap_ref[0] = jnp.where(band, keep, NEG_INF)
        cap_ref[1] = jnp.where(band & ((kj >= BLOCK) | (st > 0)), keep, NEG_INF)

    def attend(hk, carry):
        o_low_half = lax.broadcasted_iota(jnp.int32, (BLOCK, LANES), 1) < HEAD_DIM
        for n in range(ts // BLOCK):
            r0 = n * BLOCK
            q_rows = slice(r0, r0 + BLOCK)
            parts = []
            for jj in range(pairs):
                sl = slice(jj * LANES, (jj + 1) * LANES)
                parts += [ql_ref[hk, q_rows, sl], qh_ref[hk, q_rows, sl]]
            qs = jnp.concatenate(parts, axis=0)
            kb = kd_ref[hk, r0:r0 + 2 * BLOCK, :]
            s_all = lax.dot_general(qs, kb, (((1,), (1,)), ((), ())), preferred_element_type=_F32)
            weights, sink_terms = [], []
            for gq in range(Q_PER_KV):
                sink = sink_ref[hk * Q_PER_KV + gq] * LOG2_E
                s = jnp.minimum(s_all[gq * BLOCK:(gq + 1) * BLOCK, :], cap_ref[1 if n == 0 else 0])
                m = jnp.maximum(jnp.max(s, axis=-1, keepdims=True), sink)
                weights.append(jnp.exp2(s - m).astype(_BF16))
                sink_terms.append(jnp.exp2(sink - m))
            pv = _dot(jnp.concatenate(weights, axis=0), vd_ref[hk, r0:r0 + 2 * BLOCK, :])
            for jj in range(pairs):
                lo, hi = 2 * jj, 2 * jj + 1
                num = jnp.where(o_low_half, pv[lo * BLOCK:(lo + 1) * BLOCK, 0:LANES],
                                pv[hi * BLOCK:(hi + 1) * BLOCK, 0:LANES])
                den = jnp.where(o_low_half, pv[lo * BLOCK:(lo + 1) * BLOCK, LANES:] + sink_terms[lo],
                                pv[hi * BLOCK:(hi + 1) * BLOCK, LANES:] + sink_terms[hi])
                ao_ref[hk, q_rows, jj * LANES:(jj + 1) * LANES] = (num / den).astype(_BF16)
        return carry

    def project_out():
        for hk in range(N_KV_HEADS):
            am_ref[:, hk * pairs * LANES:(hk + 1) * pairs * LANES] = ao_ref[hk]
        gate = mod_ref[0, mod_row + 2:mod_row + 3, :]
        y = _dot(am_ref[...], w_out_ref[...]) + b_out_ref[...]
        r_ref[...] = DEEPNORM_ALPHA * h_ref[...] + (1.0 + gate) * y

        kd_ref[:, 0:BLOCK, :] = kd_ref[:, ts:ts + BLOCK, :]
        vd_ref[:, 0:BLOCK, :] = vd_ref[:, ts:ts + BLOCK, :]
        pool_ref[0:POOL_HALO, :] = pool_ref[ts:ts + POOL_HALO, :]

    @pl.when(t < n_tiles)
    def _():
        project_and_pool()
        lax.fori_loop(0, N_KV_HEADS, attend, 0)
        project_out()

    @pl.when(t == n_tiles)
    def _():
        finish_previous_tile()


def _rope_tables(seq):
    half = ROT_DIM // 2
    inv_freq = ROPE_THETA ** (-jnp.arange(0, ROT_DIM, 2, dtype=_F32) / ROT_DIM)
    ang = jnp.arange(seq, dtype=_F32)[:, None] * inv_freq[None, :]
    cos, sin = jnp.cos(ang), jnp.sin(ang)
    d = jnp.arange(LANES) % HEAD_DIM
    f = d % half
    c_tab = jnp.where(d[None, :] < ROT_DIM, cos[:, f], 1.0)
    s_hi = jnp.where((d[None, :] >= half) & (d[None, :] < ROT_DIM), sin[:, f], 0.0)
    s_lo = jnp.where(d[None, :] < half, -sin[:, f], 0.0)
    k_tabs = jnp.stack([c_tab, s_hi, s_lo])
    return jnp.concatenate([k_tabs * (LOG2_E * HEAD_DIM ** -0.5), k_tabs], axis=0).astype(_F32)


def _mixer_call(h2d, mods, rope, sinks, w_in, b_in, w_pool, pool_scale, w_out, b_out, ln_g, ln_b,
                *, mod_row, batch, seq):
    n, d = h2d.shape
    ts = MIX_ROWS
    tiles = seq // ts
    n_tiles = batch * tiles
    resident = functools.partial(pl.BlockSpec, pipeline_mode=pl.Buffered(1))
    const2 = lambda t: (0, 0)
    cur = lambda t: jnp.minimum(t, n_tiles - 1)
    return pl.pallas_call(
        functools.partial(_mixer_kernel, mod_row=mod_row, n_tiles=n_tiles, seq_tiles=tiles),
        grid=(n_tiles + 1,),
        in_specs=[
            pl.BlockSpec(memory_space=pltpu.SMEM),
            pl.BlockSpec((ts, d), lambda t: (cur(t), 0)),
            pl.BlockSpec((1, 3 * N_SUBLAYERS, d), lambda t: (cur(t) // tiles, 0, 0)),
            pl.BlockSpec((6, ts, LANES), lambda t: (0, cur(t) % tiles, 0)),
            resident((d, IN_PROJ_WIDTH), const2),
            pl.BlockSpec((1, IN_PROJ_WIDTH), const2),
            resident((len(POOL_WINDOWS), POOL_GROUP_WIDTH, POOL_GROUP_WIDTH), lambda t: (0, 0, 0)),
            pl.BlockSpec((1, POOL_WIDTH), const2),
            resident((d, d), const2),
            pl.BlockSpec((1, d), const2),
            pl.BlockSpec((1, d), const2),
            pl.BlockSpec((1, d), const2),
        ],
        out_specs=pl.BlockSpec((ts, d), lambda t: (jnp.maximum(t - 1, 0), 0)),
        out_shape=jax.ShapeDtypeStruct((n, d), _F32),
        scratch_shapes=[
            pltpu.VMEM((N_KV_HEADS, BLOCK + ts, LANES), _BF16),
            pltpu.VMEM((N_KV_HEADS, BLOCK + ts, 2 * LANES), _BF16),
            pltpu.VMEM((POOL_HALO + ts, POOL_WIDTH), _F32),
            pltpu.VMEM((N_KV_HEADS, ts, ATTN_WIDTH // N_KV_HEADS), _BF16),
            pltpu.VMEM((N_KV_HEADS, ts, ATTN_WIDTH // N_KV_HEADS), _BF16),
            pltpu.VMEM((N_KV_HEADS, ts, ATTN_WIDTH // N_KV_HEADS), _BF16),
            pltpu.VMEM((ts, d), _BF16),
            pltpu.VMEM((2, BLOCK, 2 * BLOCK), _F32),
            pltpu.VMEM((ts, d), _F32),
        ],
        compiler_params=pltpu.CompilerParams(
            dimension_semantics=("arbitrary",), vmem_limit_bytes=VMEM_LIMIT_BYTES),
        name="mixer_sublayer",
    )(sinks, h2d, mods, rope, w_in, b_in.reshape(1, -1), w_pool, pool_scale.reshape(1, -1),
      w_out, b_out.reshape(1, -1), ln_g.reshape(1, d), ln_b.reshape(1, d))


def kernel(x, c, w_ada, b_ada, ffn1_w1, ffn1_w3, ffn1_w2, ln1_g, ln1_b, w_in, b_in, attn_sinks, w_pool, pool_scale, w_out, b_out, ln2_g, ln2_b, ffn2_w1, ffn2_w3, ffn2_w2, ln3_g, ln3_b):
    batch, seq, d = x.shape
    assert d == D_MODEL and seq % MIX_ROWS == 0 and (batch * seq) % FFN_ROWS == 0 and seq % FFN_ROWS == 0
    assert w_ada.shape[0] == DEPTH
    h = x.reshape(batch * seq, d)
    rope = _rope_tables(seq)
    for l in range(DEPTH):
        mods = _ada_call(c, w_ada[l], b_ada[l]).reshape(batch, 3 * N_SUBLAYERS, d)
        h = _ffn_call(h, mods, ffn1_w1[l].astype(_BF16), ffn1_w3[l].astype(_BF16),
                      ffn1_w2[l].astype(_BF16), ln1_g[l], ln1_b[l], mod_row=0, seq=seq)
        h = _mixer_call(h, mods, rope, attn_sinks[l], w_in[l].astype(_BF16), b_in[l],
                        w_pool[l].astype(_BF16), pool_scale[l], w_out[l].astype(_BF16), b_out[l],
                        ln2_g[l], ln2_b[l], mod_row=3, batch=batch, seq=seq)
        h = _ffn_call(h, mods, ffn2_w1[l].astype(_BF16), ffn2_w3[l].astype(_BF16),
                      ffn2_w2[l].astype(_BF16), ln3_g[l], ln3_b[l], mod_row=6, seq=seq)
    return h.reshape(batch, seq, d)
```

```python
import functools

import jax
import jax.numpy as jnp
from jax import lax
from jax.experimental import pallas as pl
from jax.experimental.pallas import tpu as pltpu

D_MODEL = 1024
HEAD_DIM = 64
N_Q_HEADS = 8
N_KV_HEADS = 2
Q_PER_KV = N_Q_HEADS // N_KV_HEADS
ATTN_WIDTH = N_Q_HEADS * HEAD_DIM
KV_WIDTH = N_KV_HEADS * HEAD_DIM
WINDOW = 128
BLOCK = 128
ROPE_THETA = 500000.0
ROT_DIM = HEAD_DIM // 4
POOL_WINDOWS = (2, 4, 8, 16)
POOL_HALO = 16
POOL_WIDTH = D_MODEL - ATTN_WIDTH
POOL_GROUP_WIDTH = POOL_WIDTH // len(POOL_WINDOWS)
IN_PROJ_WIDTH = ATTN_WIDTH + 2 * KV_WIDTH + POOL_WIDTH
D_FF = 2816
FFN_RESIDUAL_WEIGHT = 0.5
LN_EPS = 1e-5
DEPTH = 1
DEEPNORM_ALPHA = (2.0 * DEPTH) ** 0.25
N_SUBLAYERS = 3
NEG_INF = -1e30
LOG2_E = 1.4426950408889634

LANES = 128
BF16_SUBLANES = 16
VMEM_LIMIT_BYTES = 56 * 1024 * 1024

FFN_ROWS = 512
MIX_ROWS = 1024

_BF16 = jnp.bfloat16
_F32 = jnp.float32


def _dot(a, b):
    return jnp.dot(a, b, preferred_element_type=_F32)


def _layer_norm(r, g, b):
    mu = jnp.mean(r, axis=-1, keepdims=True)
    d = r - mu
    var = jnp.mean(d * d, axis=-1, keepdims=True)
    return d * lax.rsqrt(var + LN_EPS) * g + b


def _ada_kernel(c_ref, w_ref, b_ref, o_ref):
    c = c_ref[...]
    act = (c * jax.nn.sigmoid(c)).astype(_BF16)
    o_ref[...] = _dot(act, w_ref[...].astype(_BF16)) + b_ref[...]


def _ada_call(c, w_ada, b_ada):
    batch, d = c.shape
    width = w_ada.shape[1]
    bn = D_MODEL
    return pl.pallas_call(
        _ada_kernel,
        grid=(width // bn,),
        in_specs=[
            pl.BlockSpec((batch, d), lambda j: (0, 0)),
            pl.BlockSpec((d, bn), lambda j: (0, j)),
            pl.BlockSpec((1, bn), lambda j: (0, j)),
        ],
        out_specs=pl.BlockSpec((batch, bn), lambda j: (0, j)),
        out_shape=jax.ShapeDtypeStruct((batch, width), _F32),
        compiler_params=pltpu.CompilerParams(
            dimension_semantics=("arbitrary",), vmem_limit_bytes=VMEM_LIMIT_BYTES),
        name="ada_mods",
    )(c, w_ada, b_ada.reshape(1, width))


def _ffn_kernel(*refs, mod_row, n_tiles, n_cast):
    h_ref, mod_ref, w1_ref, w3_ref, w2_ref, g_ref, b_ref = refs[:7]
    cast_in = refs[7:7 + n_cast]
    o_ref = refs[7 + n_cast]
    cast_out = refs[8 + n_cast:8 + 2 * n_cast]
    r_ref = refs[8 + 2 * n_cast]
    t = pl.program_id(0)

    @pl.when(t == 0)
    def _():
        r_ref[...] = jnp.zeros(r_ref.shape, _F32)

    def finish_previous_tile():
        o_ref[...] = _layer_norm(r_ref[...], g_ref[...], b_ref[...])
        for src, dst in zip(cast_in, cast_out):
            dst[...] = src[...].astype(_BF16)

    @pl.when(t < n_tiles)
    def _():
        finish_previous_tile()
        h = h_ref[...]
        shift = mod_ref[0, mod_row:mod_row + 1, :]
        scale = mod_ref[0, mod_row + 1:mod_row + 2, :]
        gate = mod_ref[0, mod_row + 2:mod_row + 3, :]
        u = (h * (1.0 + scale) + shift).astype(_BF16)
        a = _dot(u, w1_ref[...])
        v = _dot(u, w3_ref[...])
        act = (a * jax.nn.sigmoid(a) * v).astype(_BF16)
        f = _dot(act, w2_ref[...])
        r_ref[...] = DEEPNORM_ALPHA * h + (FFN_RESIDUAL_WEIGHT * (1.0 + gate)) * f

    @pl.when(t == n_tiles)
    def _():
        finish_previous_tile()


def _cast_chunk_spec(rows, cols, max_chunks):
    chunks = next(c for c in range(max_chunks, 0, -1)
                  if rows % c == 0 and (rows // c) % BF16_SUBLANES == 0)
    return pl.BlockSpec((rows // chunks, cols), lambda t: (jnp.minimum(t, chunks - 1), 0))


def _ffn_call(h2d, mods, w1, w3, w2, ln_g, ln_b, *, mod_row, seq, cast=()):
    n, d = h2d.shape
    tm = FFN_ROWS
    n_tiles = n // tm
    resident = functools.partial(pl.BlockSpec, pipeline_mode=pl.Buffered(1))
    cur = lambda t: jnp.minimum(t, n_tiles - 1)
    prev = lambda t: jnp.maximum(t - 1, 0)
    cast_specs = [_cast_chunk_spec(*a.shape, n_tiles) for a in cast]
    out, *narrowed = pl.pallas_call(
        functools.partial(_ffn_kernel, mod_row=mod_row, n_tiles=n_tiles, n_cast=len(cast)),
        grid=(n_tiles + 1,),
        in_specs=[
            pl.BlockSpec((tm, d), lambda t: (cur(t), 0)),
            pl.BlockSpec((1, 3 * N_SUBLAYERS, d), lambda t: (cur(t) * tm // seq, 0, 0)),
            resident((d, D_FF), lambda t: (0, 0)),
            resident((d, D_FF), lambda t: (0, 0)),
            resident((D_FF, d), lambda t: (0, 0)),
            pl.BlockSpec((1, d), lambda t: (0, 0)),
            pl.BlockSpec((1, d), lambda t: (0, 0)),
        ] + cast_specs,
        out_specs=[pl.BlockSpec((tm, d), lambda t: (prev(t), 0))] + cast_specs,
        out_shape=[jax.ShapeDtypeStruct((n, d), _F32)]
        + [jax.ShapeDtypeStruct(a.shape, _BF16) for a in cast],
        scratch_shapes=[pltpu.VMEM((tm, d), _F32)],
        compiler_params=pltpu.CompilerParams(
            dimension_semantics=("arbitrary",), vmem_limit_bytes=VMEM_LIMIT_BYTES),
        name=f"ffn_sublayer_{mod_row // 3}",
    )(h2d, mods, w1, w3, w2, ln_g.reshape(1, d), ln_b.reshape(1, d), *cast)
    return out, narrowed


def _rope(t, cos, sin_hi, sin_lo):
    half = ROT_DIM // 2
    return t * cos + pltpu.roll(t, half, 1) * sin_hi + pltpu.roll(t, LANES - half, 1) * sin_lo


def _mixer_kernel(sink_ref, h_ref, mod_ref, rope_ref, w_in_ref, b_in_ref, w_pool_ref, pscale_ref,
                  w_out_ref, b_out_ref, g_ref, b_ref, o_ref,
                  kd_ref, vd_ref, pool_ref, ql_ref, qh_ref, ao_ref, am_ref, cap_ref, r_ref,
                  *, mod_row, n_tiles, seq_tiles):
    ts = h_ref.shape[0]
    t = pl.program_id(0)
    st = t % seq_tiles

    @pl.when(t == 0)
    def _():
        r_ref[...] = jnp.zeros(r_ref.shape, _F32)

    @pl.when(st == 0)
    def _():
        kd_ref[:, 0:BLOCK, :] = jnp.zeros((N_KV_HEADS, BLOCK, LANES), _BF16)
        vd_ref[:, 0:BLOCK, :] = jnp.zeros((N_KV_HEADS, BLOCK, 2 * LANES), _BF16)
        pool_ref[0:POOL_HALO, :] = jnp.zeros((POOL_HALO, POOL_WIDTH), _F32)

    pairs = Q_PER_KV // 2

    def finish_previous_tile():
        o_ref[...] = _layer_norm(r_ref[...], g_ref[...], b_ref[...])

    def project_and_pool():
        h = h_ref[...]
        shift = mod_ref[0, mod_row:mod_row + 1, :]
        scale = mod_ref[0, mod_row + 1:mod_row + 2, :]
        u = (h * (1.0 + scale) + shift).astype(_BF16)

        lane = lax.broadcasted_iota(jnp.int32, (ts, LANES), 1)
        low_half = lane < HEAD_DIM

        zq = _dot(u, w_in_ref[:, 0:ATTN_WIDTH]) + b_in_ref[:, 0:ATTN_WIDTH]
        finish_previous_tile()
        for j in range(ATTN_WIDTH // LANES):
            tr = _rope(zq[:, j * LANES:(j + 1) * LANES], rope_ref[0], rope_ref[1], rope_ref[2])
            hk, sl = j // pairs, slice((j % pairs) * LANES, (j % pairs + 1) * LANES)
            ql_ref[hk, :, sl] = jnp.where(low_half, tr, 0.0).astype(_BF16)
            qh_ref[hk, :, sl] = jnp.where(low_half, 0.0, tr).astype(_BF16)

        zkv = _dot(u, w_in_ref[:, ATTN_WIDTH:ATTN_WIDTH + 2 * KV_WIDTH]) \
            + b_in_ref[:, ATTN_WIDTH:ATTN_WIDTH + 2 * KV_WIDTH]
        kr = _rope(zkv[:, 0:KV_WIDTH], rope_ref[3], rope_ref[4], rope_ref[5])
        vv = zkv[:, KV_WIDTH:2 * KV_WIDTH]
        for src, dst in ((kr, kd_ref), (vv, vd_ref)):
            swapped = pltpu.roll(src, HEAD_DIM, 1)
            dst[0, BLOCK:BLOCK + ts, 0:LANES] = jnp.where(low_half, src, swapped).astype(_BF16)
            dst[1, BLOCK:BLOCK + ts, 0:LANES] = jnp.where(low_half, swapped, src).astype(_BF16)
        vd_ref[:, BLOCK:BLOCK + ts, LANES:2 * LANES] = jnp.ones((N_KV_HEADS, ts, LANES), _BF16)

        pool_ref[POOL_HALO:POOL_HALO + ts, :] = _dot(u, w_in_ref[:, ATTN_WIDTH + 2 * KV_WIDTH:]) \
            + b_in_ref[:, ATTN_WIDTH + 2 * KV_WIDTH:]
        head_pos = st * ts + lax.broadcasted_iota(jnp.int32, (POOL_HALO, 1), 0)
        for g, w in enumerate(POOL_WINDOWS):
            sl = slice(g * POOL_GROUP_WIDTH, (g + 1) * POOL_GROUP_WIDTH)
            ext = pool_ref[:, sl]
            acc = ext
            k = 1
            while k < w:
                acc = acc + pltpu.roll(acc, k, 0)
                k *= 2
            head_count = jnp.minimum(head_pos + 1, w).astype(_F32)
            mean = jnp.concatenate([acc[POOL_HALO:2 * POOL_HALO, :] / head_count,
                                    acc[2 * POOL_HALO:, :] * (1.0 / w)], axis=0)
            diff = (mean - ext[POOL_HALO:, :]).astype(_BF16)
            mixed = _dot(diff, w_pool_ref[g]) * pscale_ref[:, sl]
            am_ref[:, ATTN_WIDTH + g * POOL_GROUP_WIDTH:ATTN_WIDTH + (g + 1) * POOL_GROUP_WIDTH] = \
                mixed.astype(_BF16)

        qi = lax.broadcasted_iota(jnp.int32, (BLOCK, 2 * BLOCK), 0)
        kj = lax.broadcasted_iota(jnp.int32, (BLOCK, 2 * BLOCK), 1)
        band = (kj > qi) & (kj <= qi + WINDOW)
        keep = float(jnp.finfo(_F32).max)
        cap_ref[0] = jnp.where(band, keep, NEG_INF)
        cap_ref[1] = jnp.where(band & ((kj >= BLOCK) | (st > 0)), keep, NEG_INF)

    def attend(hk, carry):
        o_low_half = lax.broadcasted_iota(jnp.int32, (BLOCK, LANES), 1) < HEAD_DIM
        for n in range(ts // BLOCK):
            r0 = n * BLOCK
            q_rows = slice(r0, r0 + BLOCK)
            parts = []
            for jj in range(pairs):
                sl = slice(jj * LANES, (jj + 1) * LANES)
                parts += [ql_ref[hk, q_rows, sl], qh_ref[hk, q_rows, sl]]
            qs = jnp.concatenate(parts, axis=0)
            kb = kd_ref[hk, r0:r0 + 2 * BLOCK, :]
            s_all = lax.dot_general(qs, kb, (((1,), (1,)), ((), ())), preferred_element_type=_F32)
            weights, sink_terms = [], []
            for gq in range(Q_PER_KV):
                sink = sink_ref[hk * Q_PER_KV + gq] * LOG2_E
                s = jnp.minimum(s_all[gq * BLOCK:(gq + 1) * BLOCK, :], cap_ref[1 if n == 0 else 0])
                m = jnp.maximum(jnp.max(s, axis=-1, keepdims=True), sink)
                weights.append(jnp.exp2(s - m).astype(_BF16))
                sink_terms.append(jnp.exp2(sink - m))
            pv = _dot(jnp.concatenate(weights, axis=0), vd_ref[hk, r0:r0 + 2 * BLOCK, :])
            for jj in range(pairs):
                lo, hi = 2 * jj, 2 * jj + 1
                num = jnp.where(o_low_half, pv[lo * BLOCK:(lo + 1) * BLOCK, 0:LANES],
                                pv[hi * BLOCK:(hi + 1) * BLOCK, 0:LANES])
                den = jnp.where(o_low_half, pv[lo * BLOCK:(lo + 1) * BLOCK, LANES:] + sink_terms[lo],
                                pv[hi * BLOCK:(hi + 1) * BLOCK, LANES:] + sink_terms[hi])
                ao_ref[hk, q_rows, jj * LANES:(jj + 1) * LANES] = (num / den).astype(_BF16)
        return carry

    def project_out():
        for hk in range(N_KV_HEADS):
            am_ref[:, hk * pairs * LANES:(hk + 1) * pairs * LANES] = ao_ref[hk]
        gate = mod_ref[0, mod_row + 2:mod_row + 3, :]
        y = _dot(am_ref[...], w_out_ref[...]) + b_out_ref[...]
        r_ref[...] = DEEPNORM_ALPHA * h_ref[...] + (1.0 + gate) * y

        kd_ref[:, 0:BLOCK, :] = kd_ref[:, ts:ts + BLOCK, :]
        vd_ref[:, 0:BLOCK, :] = vd_ref[:, ts:ts + BLOCK, :]
        pool_ref[0:POOL_HALO, :] = pool_ref[ts:ts + POOL_HALO, :]

    @pl.when(t < n_tiles)
    def _():
        project_and_pool()
        lax.fori_loop(0, N_KV_HEADS, attend, 0)
        project_out()

    @pl.when(t == n_tiles)
    def _():
        finish_previous_tile()


def _rope_tables(seq):
    half = ROT_DIM // 2
    inv_freq = ROPE_THETA ** (-jnp.arange(0, ROT_DIM, 2, dtype=_F32) / ROT_DIM)
    ang = jnp.arange(seq, dtype=_F32)[:, None] * inv_freq[None, :]
    cos, sin = jnp.cos(ang), jnp.sin(ang)
    d = jnp.arange(LANES) % HEAD_DIM
    f = d % half
    c_tab = jnp.where(d[None, :] < ROT_DIM, cos[:, f], 1.0)
    s_hi = jnp.where((d[None, :] >= half) & (d[None, :] < ROT_DIM), sin[:, f], 0.0)
    s_lo = jnp.where(d[None, :] < half, -sin[:, f], 0.0)
    k_tabs = jnp.stack([c_tab, s_hi, s_lo])
    return jnp.concatenate([k_tabs * (LOG2_E * HEAD_DIM ** -0.5), k_tabs], axis=0).astype(_F32)


def _mixer_call(h2d, mods, rope, sinks, w_in, b_in, w_pool, pool_scale, w_out, b_out, ln_g, ln_b,
                *, mod_row, batch, seq):
    n, d = h2d.shape
    ts = MIX_ROWS
    tiles = seq // ts
    n_tiles = batch * tiles
    resident = functools.partial(pl.BlockSpec, pipeline_mode=pl.Buffered(1))
    const2 = lambda t: (0, 0)
    cur = lambda t: jnp.minimum(t, n_tiles - 1)
    return pl.pallas_call(
        functools.partial(_mixer_kernel, mod_row=mod_row, n_tiles=n_tiles, seq_tiles=tiles),
        grid=(n_tiles + 1,),
        in_specs=[
            pl.BlockSpec(memory_space=pltpu.SMEM),
            pl.BlockSpec((ts, d), lambda t: (cur(t), 0)),
            pl.BlockSpec((1, 3 * N_SUBLAYERS, d), lambda t: (cur(t) // tiles, 0, 0)),
            pl.BlockSpec((6, ts, LANES), lambda t: (0, cur(t) % tiles, 0)),
            resident((d, IN_PROJ_WIDTH), const2),
            pl.BlockSpec((1, IN_PROJ_WIDTH), const2),
            resident((len(POOL_WINDOWS), POOL_GROUP_WIDTH, POOL_GROUP_WIDTH), lambda t: (0, 0, 0)),
            pl.BlockSpec((1, POOL_WIDTH), const2),
            resident((d, d), const2),
            pl.BlockSpec((1, d), const2),
            pl.BlockSpec((1, d), const2),
            pl.BlockSpec((1, d), const2),
        ],
        out_specs=pl.BlockSpec((ts, d), lambda t: (jnp.maximum(t - 1, 0), 0)),
        out_shape=jax.ShapeDtypeStruct((n, d), _F32),
        scratch_shapes=[
            pltpu.VMEM((N_KV_HEADS, BLOCK + ts, LANES), _BF16),
            pltpu.VMEM((N_KV_HEADS, BLOCK + ts, 2 * LANES), _BF16),
            pltpu.VMEM((POOL_HALO + ts, POOL_WIDTH), _F32),
            pltpu.VMEM((N_KV_HEADS, ts, ATTN_WIDTH // N_KV_HEADS), _BF16),
            pltpu.VMEM((N_KV_HEADS, ts, ATTN_WIDTH // N_KV_HEADS), _BF16),
            pltpu.VMEM((N_KV_HEADS, ts, ATTN_WIDTH // N_KV_HEADS), _BF16),
            pltpu.VMEM((ts, d), _BF16),
            pltpu.VMEM((2, BLOCK, 2 * BLOCK), _F32),
            pltpu.VMEM((ts, d), _F32),
        ],
        compiler_params=pltpu.CompilerParams(
            dimension_semantics=("arbitrary",), vmem_limit_bytes=VMEM_LIMIT_BYTES),
        name="mixer_sublayer",
    )(sinks, h2d, mods, rope, w_in, b_in.reshape(1, -1), w_pool, pool_scale.reshape(1, -1),
      w_out, b_out.reshape(1, -1), ln_g.reshape(1, d), ln_b.reshape(1, d))


def kernel(x, c, w_ada, b_ada, ffn1_w1, ffn1_w3, ffn1_w2, ln1_g, ln1_b, w_in, b_in, attn_sinks, w_pool, pool_scale, w_out, b_out, ln2_g, ln2_b, ffn2_w1, ffn2_w3, ffn2_w2, ln3_g, ln3_b):
    batch, seq, d = x.shape
    assert d == D_MODEL and seq % MIX_ROWS == 0 and (batch * seq) % FFN_ROWS == 0 and seq % FFN_ROWS == 0
    assert w_ada.shape[0] == DEPTH
    h = x.reshape(batch * seq, d)
    rope = _rope_tables(seq)
    for l in range(DEPTH):
        mods = _ada_call(c, w_ada[l], b_ada[l]).reshape(batch, 3 * N_SUBLAYERS, d)
        later = [w_in[l], w_pool[l].reshape(-1, POOL_GROUP_WIDTH), w_out[l],
                 ffn2_w1[l], ffn2_w3[l], ffn2_w2[l]]
        h, (w_in_b, w_pool_b, w_out_b, w1_b, w3_b, w2_b) = _ffn_call(
            h, mods, ffn1_w1[l].astype(_BF16), ffn1_w3[l].astype(_BF16), ffn1_w2[l].astype(_BF16),
            ln1_g[l], ln1_b[l], mod_row=0, seq=seq, cast=later)
        h = _mixer_call(h, mods, rope, attn_sinks[l], w_in_b, b_in[l],
                        w_pool_b.reshape(w_pool[l].shape), pool_scale[l], w_out_b, b_out[l],
                        ln2_g[l], ln2_b[l], mod_row=3, batch=batch, seq=seq)
        h, _ = _ffn_call(h, mods, w1_b, w3_b, w2_b, ln3_g[l], ln3_b[l], mod_row=6, seq=seq)
    return h.reshape(batch, seq, d)
```
